```python
import math
import jax, jax.numpy as jnp
from jax import lax
import numpy as np

D_MODEL = 1024
BATCH = 8
SEQ = 2048
DEPTH = 2

CHUNK = 64
N_META = 16
BLOCK_Q = 128
N_PAD = BLOCK_Q - N_META
N_MIXERS = 2
D_FF = 2816
RMS_EPS = 1e-6
NEG_INF = -1e30
DA_HEADS = 8
DA_HEAD_DIM = D_MODEL // DA_HEADS // 2
DA_V_DIM = 2 * DA_HEAD_DIM
DA_QK_WIDTH = DA_HEADS * 2 * DA_HEAD_DIM
DA_IN_WIDTH = 2 * DA_QK_WIDTH + DA_HEADS * DA_V_DIM
GLA_HEADS = 4
GLA_KEY_WIDTH = D_MODEL // 2
GLA_VAL_WIDTH = D_MODEL
GLA_DK = GLA_KEY_WIDTH // GLA_HEADS
GLA_DV = GLA_VAL_WIDTH // GLA_HEADS
GLA_GATE_RANK = 16
GLA_GATE_NORM = 16.0
GLA_IN_WIDTH = 2 * GLA_KEY_WIDTH + 2 * GLA_VAL_WIDTH + GLA_GATE_RANK
N_A = (DEPTH + 1) // 2
N_B = DEPTH // 2

kernel_name = "hybrid_diffattn_gla_macaron"


def rmsnorm(x, g):
    xf = x.astype(jnp.float32)
    y = xf * lax.rsqrt(jnp.mean(xf * xf, axis=-1, keepdims=True) + RMS_EPS)
    return (y * g.astype(jnp.float32)).astype(x.dtype)


def swiglu(h, w_in, w_out):
    gu = h @ w_in
    g, u = gu[..., :D_FF], gu[..., D_FF:]
    return (jax.nn.silu(g) * u) @ w_out


def pad_front(h):
    return jnp.pad(h, ((0, 0), (N_PAD, 0), (0, 0)))


def diff_attention(h, w_in, lam_p, subln_g, w_out, lam_init):
    B, L, _ = h.shape
    hp = pad_front(h)
    P = hp.shape[1]
    qkv = hp @ w_in
    q = qkv[..., :DA_QK_WIDTH].reshape(B, P, DA_HEADS, 2, DA_HEAD_DIM).transpose(0, 3, 2, 1, 4)
    k = qkv[..., DA_QK_WIDTH:2 * DA_QK_WIDTH].reshape(B, P, DA_HEADS, 2, DA_HEAD_DIM).transpose(0, 3, 2, 1, 4)
    v = qkv[..., 2 * DA_QK_WIDTH:].reshape(B, P, DA_HEADS, DA_V_DIM).transpose(0, 2, 1, 3)
    lp = lam_p.astype(jnp.float32)
    lam = jnp.exp(jnp.sum(lp[0] * lp[1])) - jnp.exp(jnp.sum(lp[2] * lp[3])) + lam_init
    pos = jnp.arange(P)
    chunk_id = pos // CHUNK
    k_valid = pos >= N_PAD
    scale = DA_HEAD_DIM ** -0.5
    outs = []
    for s in range(0, P, BLOCK_Q):
        e = s + BLOCK_Q
        sc = jnp.einsum('bmhqd,bmhkd->bmhqk', q[:, :, :, s:e], k[:, :, :, :e]).astype(jnp.float32) * scale
        allowed = k_valid[None, :e] & (chunk_id[None, :e] <= chunk_id[s:e, None])
        p = jax.nn.softmax(jnp.where(allowed, sc, NEG_INF), axis=-1)
        a = p[:, 0] - lam * p[:, 1]
        outs.append(jnp.einsum('bhqk,bhkv->bhqv', a.astype(v.dtype), v[:, :, :e]))
    o = jnp.concatenate(outs, axis=2)[:, :, N_PAD:]
    o = rmsnorm(o, subln_g) * (1.0 - lam_init)
    return o.transpose(0, 2, 1, 3).reshape(B, L, DA_HEADS * DA_V_DIM) @ w_out


def gla_mixer(h, w_in, w_gate2, b_gate2, norm_g, w_out):
    B, L, _ = h.shape
    hp = pad_front(h)
    P = hp.shape[1]
    NC = P // CHUNK
    proj = hp @ w_in
    o1 = GLA_KEY_WIDTH
    o2 = 2 * GLA_KEY_WIDTH
    o3 = o2 + GLA_VAL_WIDTH
    o4 = o3 + GLA_VAL_WIDTH
    q, k, v, og, glr = proj[..., :o1], proj[..., o1:o2], proj[..., o2:o3], proj[..., o3:o4], proj[..., o4:]
    gk = jax.nn.log_sigmoid((glr @ w_gate2 + b_gate2).astype(jnp.float32)) / GLA_GATE_NORM
    valid = (jnp.arange(P) >= N_PAD)[None, :, None]
    gk = jnp.where(valid, gk, 0.0)
    q = (q.astype(jnp.float32) * GLA_DK ** -0.5).reshape(B, NC, CHUNK, GLA_HEADS, GLA_DK)
    k = k.astype(jnp.float32).reshape(B, NC, CHUNK, GLA_HEADS, GLA_DK)
    v = v.astype(jnp.float32).reshape(B, NC, CHUNK, GLA_HEADS, GLA_DV)
    G = jnp.cumsum(gk.reshape(B, NC, CHUNK, GLA_HEADS, GLA_DK), axis=2)
    G_end = G[:, :, -1:]
    k_dec = k * jnp.exp(G_end - G)
    kv = jnp.einsum('bnchk,bnchv->bnhkv', k_dec, v)
    decay = jnp.exp(G_end[:, :, 0])

    def step(S, inp):
        kv_c, d_c, q_c = inp
        S = d_c[..., None] * S + kv_c
        return S, jnp.einsum('bchk,bhkv->bchv', q_c, S)

    S0 = jnp.zeros((B, GLA_HEADS, GLA_DK, GLA_DV), jnp.float32)
    _, o = lax.scan(step, S0, (jnp.moveaxis(kv, 1, 0), jnp.moveaxis(decay, 1, 0), jnp.moveaxis(q, 1, 0)))
    o = jnp.moveaxis(o, 0, 1).reshape(B, P, GLA_HEADS, GLA_DV)[:, N_PAD:].astype(h.dtype)
    gate = jax.nn.silu(og[:, N_PAD:].reshape(B, L, GLA_HEADS, GLA_DV))
    o = rmsnorm(o, norm_g) * gate
    return o.reshape(B, L, GLA_VAL_WIDTH) @ w_out


def setup_inputs(seed: int = 0) -> dict:
    key = jax.random.key(seed)
    ks = jax.random.split(key, 24)
    nrm = lambda k, shape, fan_in: jax.random.normal(k, shape, jnp.float32) * fan_in ** -0.5
    gain = lambda k, shape: 1.0 + 0.02 * jax.random.normal(k, shape, jnp.float32)
    return {
        "x": jax.random.normal(ks[0], (BATCH, SEQ, D_MODEL), jnp.float32),
        "meta": jax.random.normal(ks[1], (N_META, D_MODEL), jnp.float32),
        "ffn1_norm": gain(ks[2], (DEPTH, D_MODEL)),
        "ffn1_w_in": nrm(ks[3], (DEPTH, D_MODEL, 2 * D_FF), D_MODEL),
        "ffn1_w_out": nrm(ks[4], (DEPTH, D_FF, D_MODEL), D_FF),
        "mix_norm": gain(ks[5], (DEPTH, D_MODEL)),
        "ffn2_norm": gain(ks[6], (DEPTH, D_MODEL)),
        "ffn2_w_in": nrm(ks[7], (DEPTH, D_MODEL, 2 * D_FF), D_MODEL),
        "ffn2_w_out": nrm(ks[8], (DEPTH, D_FF, D_MODEL), D_FF),
        "da_w_in": nrm(ks[9], (N_A, D_MODEL, DA_IN_WIDTH), D_MODEL),
        "da_lambda": 0.1 * jax.random.normal(ks[10], (N_A, 4, DA_HEAD_DIM), jnp.float32),
        "da_subln": gain(ks[11], (N_A, DA_V_DIM)),
        "da_w_out": nrm(ks[12], (N_A, DA_HEADS * DA_V_DIM, D_MODEL), DA_HEADS * DA_V_DIM),
        "gla_w_in": nrm(ks[13], (N_B, D_MODEL, GLA_IN_WIDTH), D_MODEL),
        "gla_w_gate2": nrm(ks[14], (N_B, GLA_GATE_RANK, GLA_KEY_WIDTH), GLA_GATE_RANK),
        "gla_b_gate2": 0.1 * jax.random.normal(ks[15], (N_B, GLA_KEY_WIDTH), jnp.float32),
        "gla_norm": gain(ks[16], (N_B, GLA_DV)),
        "gla_w_out": nrm(ks[17], (N_B, GLA_VAL_WIDTH, D_MODEL), GLA_VAL_WIDTH),
        "final_norm": gain(ks[18], (D_MODEL,)),
    }


def reference(x, meta, ffn1_norm, ffn1_w_in, ffn1_w_out, mix_norm, ffn2_norm, ffn2_w_in, ffn2_w_out,
              da_w_in, da_lambda, da_subln, da_w_out,
              gla_w_in, gla_w_gate2, gla_b_gate2, gla_norm, gla_w_out, final_norm):
    B = x.shape[0]
    meta_b = jnp.broadcast_to(meta[None].astype(x.dtype), (B, N_META, D_MODEL))
    h = jnp.concatenate([meta_b, x], axis=1)
    for i in range(DEPTH):
        h = h + 0.5 * swiglu(rmsnorm(h, ffn1_norm[i]), ffn1_w_in[i], ffn1_w_out[i])
        hn = rmsnorm(h, mix_norm[i])
        j = i // N_MIXERS
        if i % N_MIXERS == 0:
            lam_init = 0.8 - 0.6 * math.exp(-0.3 * i)
            h = h + diff_attention(hn, da_w_in[j], da_lambda[j], da_subln[j], da_w_out[j], lam_init)
        else:
            h = h + gla_mixer(hn, gla_w_in[j], gla_w_gate2[j], gla_b_gate2[j], gla_norm[j], gla_w_out[j])
        h = h + 0.5 * swiglu(rmsnorm(h, ffn2_norm[i]), ffn2_w_in[i], ffn2_w_out[i])
    return rmsnorm(h, final_norm)[:, N_META:]
```

```python
import functools
import math

import jax
import jax.numpy as jnp
from jax import lax
from jax.experimental import pallas as pl
from jax.experimental.pallas import tpu as pltpu

F32 = jnp.float32
BF16 = jnp.bfloat16

D_MODEL = 1024
BATCH = 8
SEQ = 2048
N_META = 16
D_FF = 2816
RMS_EPS = 1e-6
NEG_INF = -1e30
CHUNK = 64
BLOCK_Q = 128

DA_HEADS = 8
DA_HEAD_DIM = 64
DA_V_DIM = 128
DA_QK_WIDTH = 1024

GLA_HEADS = 4
GLA_KEY_WIDTH = 512
GLA_VAL_WIDTH = 1024
GLA_DK = 128
GLA_DV = 256
GLA_GATE_RANK = 16
GLA_GATE_NORM = 16.0

X_ROWS = BATCH * SEQ
M_ROWS = BATCH * N_META
LANES = 128

ROW_TILE = 512
FF_TILE = 256
PROJ_TILE = 512
VMEM_LIMIT = 56 * 1024 * 1024


def _rms(x, g):
    return x * lax.rsqrt(jnp.mean(x * x, axis=-1, keepdims=True) + RMS_EPS) * g


def _dot(a, b):
    return jnp.dot(a, b, preferred_element_type=F32)


def _dot_nt(a, b):
    return lax.dot_general(a, b, (((1,), (1,)), ((), ())), preferred_element_type=F32)


def _dot_tn(a, b):
    return lax.dot_general(a, b, (((0,), (0,)), ((), ())), preferred_element_type=F32)


def _rowwise_call(body, xs, ms, consts, out_specs_wd, scratch_wd, name, with_meta=True):
    bm = ROW_TILE
    nx = X_ROWS // bm
    n_in, n_out, n_c = len(xs), len(out_specs_wd), len(consts)
    xmap = lambda i: (jnp.minimum(i, nx - 1), 0)
    cmap = lambda i: (0, 0)
    in_specs = [pl.BlockSpec((bm, a.shape[1]), xmap) for a in xs]
    args = list(xs)
    if with_meta:
        in_specs += [pl.BlockSpec((M_ROWS, a.shape[1]), cmap) for a in ms]
        args += list(ms)
    in_specs += [pl.BlockSpec(c.shape, cmap, pipeline_mode=pl.Buffered(1)) for c in consts]
    args += list(consts)
    out_specs = [pl.BlockSpec((bm, w), xmap) for w, _ in out_specs_wd]
    out_shape = [jax.ShapeDtypeStruct((X_ROWS, w), dt) for w, dt in out_specs_wd]
    if with_meta:
        out_specs += [pl.BlockSpec((M_ROWS, w), cmap) for w, _ in out_specs_wd]
        out_shape += [jax.ShapeDtypeStruct((M_ROWS, w), dt) for w, dt in out_specs_wd]
    n_m = n_in if with_meta else 0
    n_om = n_out if with_meta else 0

    def kern(*refs):
        x_refs = refs[:n_in]
        m_refs = refs[n_in:n_in + n_m]
        c_refs = refs[n_in + n_m:n_in + n_m + n_c]
        o0 = n_in + n_m + n_c
        ox_refs = refs[o0:o0 + n_out]
        om_refs = refs[o0 + n_out:o0 + n_out + n_om]
        s_refs = refs[o0 + n_out + n_om:]
        i = pl.program_id(0)
        if with_meta:
            @pl.when(i < nx)
            def _():
                body(x_refs, c_refs, ox_refs, s_refs, bm)

            @pl.when(i == nx)
            def _():
                body(m_refs, c_refs, om_refs, s_refs, M_ROWS)
        else:
            body(x_refs, c_refs, ox_refs, s_refs, bm)

    return pl.pallas_call(
        kern,
        grid=(nx + (1 if with_meta else 0),),
        in_specs=in_specs,
        out_specs=out_specs,
        out_shape=out_shape,
        scratch_shapes=[pltpu.VMEM((bm, w), dt) for w, dt in scratch_wd],
        compiler_params=pltpu.CompilerParams(
            dimension_semantics=("arbitrary",), vmem_limit_bytes=VMEM_LIMIT),
        name=name,
    )(*args)


def _ffn_body(final, ins, cs, outs, scr, rows):
    (h_ref,), (out_ref,), (act_ref,) = ins, outs, scr
    g_ref, win_ref, wout_ref = cs[:3]
    xn = _rms(h_ref[...], g_ref[...]).astype(BF16)
    for j in range(D_FF // FF_TILE):
        lo = j * FF_TILE
        gate = _dot(xn, win_ref[:, lo:lo + FF_TILE])
        up = _dot(xn, win_ref[:, D_FF + lo:D_FF + lo + FF_TILE])
        act_ref[0:rows, lo:lo + FF_TILE] = (gate * jax.nn.sigmoid(gate) * up).astype(BF16)
    y = h_ref[...] + 0.5 * _dot(act_ref[0:rows, :], wout_ref[...])
    if final:
        y = _rms(y, cs[3][...])
    out_ref[...] = y


def _ffn(hx, hm, g, w_in, w_out, final_g=None):
    final = final_g is not None
    consts = [g, w_in, w_out] + ([final_g] if final else [])
    return _rowwise_call(functools.partial(_ffn_body, final), [hx], [hm], consts,
                         [(D_MODEL, F32)], [(D_FF, BF16)],
                         "ffn_final" if final else "ffn", with_meta=not final)


def _proj_body(q_scale, q_width, ins, cs, outs, scr, rows):
    (h_ref,), (out_ref,) = ins, outs
    g_ref, w_ref = cs
    xn = _rms(h_ref[...], g_ref[...]).astype(BF16)
    width = w_ref.shape[1]
    for lo in range(0, width, PROJ_TILE):
        y = _dot(xn, w_ref[:, lo:lo + PROJ_TILE])
        if lo < q_width:
            y = y * q_scale
        out_ref[:, lo:lo + PROJ_TILE] = y.astype(BF16)


def _gla_proj_body(ins, cs, outs, scr, rows):
    (h_ref,), (out_ref, gk_ref) = ins, outs
    g_ref, w_ref, wlr_ref, wg2_ref, bg2_ref = cs
    xn = _rms(h_ref[...], g_ref[...]).astype(BF16)
    for lo in range(0, w_ref.shape[1], PROJ_TILE):
        y = _dot(xn, w_ref[:, lo:lo + PROJ_TILE])
        if lo < GLA_KEY_WIDTH:
            y = y * (GLA_DK ** -0.5)
        out_ref[:, lo:lo + PROJ_TILE] = y.astype(BF16)
    glr = _dot(xn, wlr_ref[...]).astype(BF16)
    z = _dot(glr, wg2_ref[...]) + bg2_ref[...]
    log_sig = jnp.minimum(z, 0.0) - jnp.log1p(jnp.exp(-jnp.abs(z)))
    gk_ref[...] = log_sig / GLA_GATE_NORM


def _outproj_body(ins, cs, outs, scr, rows):
    (o_ref, h_ref), (out_ref,), (w_ref,) = ins, outs, cs
    out_ref[...] = h_ref[...] + _dot(o_ref[...], w_ref[...])


def _attn_kernel(lam_init, q_ref, k_ref, v_ref, qm_ref, km_ref, vm_ref, lam_ref, sg_ref,
                 o_ref, om_ref, kmeta_ref, vmeta_ref):
    lam = (jnp.exp(jnp.sum(lam_ref[0:1, :] * lam_ref[1:2, :], axis=-1, keepdims=True))
           - jnp.exp(jnp.sum(lam_ref[2:3, :] * lam_ref[3:4, :], axis=-1, keepdims=True))
           + lam_init)
    sg = sg_ref[...] * (1.0 - lam_init)

    kmeta_ref[...] = jnp.zeros_like(kmeta_ref)
    vmeta_ref[...] = jnp.zeros_like(vmeta_ref)
    kmeta_ref[0:N_META, :] = km_ref[...]
    vmeta_ref[0:N_META, :] = vm_ref[...]
    kmeta = kmeta_ref[...]
    vmeta = vmeta_ref[...]

    def stack_maps(qb):
        n = qb.shape[0]
        first = lax.broadcasted_iota(jnp.int32, (n, LANES), 1) < DA_HEAD_DIM
        qs = qb.astype(F32) * (DA_HEAD_DIM ** -0.5)
        zero = jnp.zeros_like(qs)
        return jnp.concatenate([jnp.where(first, qs, zero).astype(BF16),
                                jnp.where(first, zero, qs).astype(BF16)], axis=0)

    def attend(qs, pieces):
        n = qs.shape[0] // 2
        scores = []
        for keys, _, allowed in pieces:
            s = _dot_nt(qs, keys)
            if allowed is not None:
                s = jnp.where(allowed, s, NEG_INF)
            scores.append(s)
        m = scores[0].max(axis=-1, keepdims=True)
        for s in scores[1:]:
            m = jnp.maximum(m, s.max(axis=-1, keepdims=True))
        probs = [jnp.exp(s - m) for s in scores]
        den = probs[0].sum(axis=-1, keepdims=True)
        for p in probs[1:]:
            den = den + p.sum(axis=-1, keepdims=True)
        r = 1.0 / den
        c1 = r[0:n]
        c2 = lam * r[n:2 * n]
        o = None
        for p, (_, values, _) in zip(probs, pieces):
            a = (p[0:n] * c1 - p[n:2 * n] * c2).astype(BF16)
            pv = _dot(a, values)
            o = pv if o is None else o + pv
        return (_rms(o, sg)).astype(BF16)

    col = lax.broadcasted_iota(jnp.int32, (2 * BLOCK_Q, BLOCK_Q), 1)
    row = lax.broadcasted_iota(jnp.int32, (2 * BLOCK_Q, BLOCK_Q), 0) & (BLOCK_Q - 1)
    meta_allowed = col < N_META
    diag_allowed = jnp.logical_not((row < CHUNK) & (col >= CHUNK))

    for j in range(SEQ // BLOCK_Q):
        s0 = j * BLOCK_Q
        qs = stack_maps(q_ref[s0:s0 + BLOCK_Q, :])
        pieces = [(kmeta, vmeta, meta_allowed)]
        if j > 0:
            pieces.append((k_ref[0:s0, :], v_ref[0:s0, :], None))
        pieces.append((k_ref[s0:s0 + BLOCK_Q, :], v_ref[s0:s0 + BLOCK_Q, :], diag_allowed))
        o_ref[s0:s0 + BLOCK_Q, :] = attend(qs, pieces)

    qs_m = stack_maps(qm_ref[...])
    col_m = lax.broadcasted_iota(jnp.int32, (2 * N_META, BLOCK_Q), 1)
    om_ref[...] = attend(qs_m, [(kmeta, vmeta, col_m < N_META)])


def _attn(qkv_x, qkv_m, lam_p, subln_g, lam_init):
    nqk = DA_QK_WIDTH // LANES
    xspec = lambda off: pl.BlockSpec((SEQ, LANES), lambda b, h: (b, off + h))
    mspec = lambda off: pl.BlockSpec((N_META, LANES), lambda b, h: (b, off + h))
    cmap = lambda b, h: (0, 0)
    return pl.pallas_call(
        functools.partial(_attn_kernel, lam_init),
        grid=(BATCH, DA_HEADS),
        in_specs=[xspec(0), xspec(nqk), xspec(2 * nqk), mspec(0), mspec(nqk), mspec(2 * nqk),
                  pl.BlockSpec(lam_p.shape, cmap), pl.BlockSpec(subln_g.shape, cmap)],
        out_specs=[pl.BlockSpec((SEQ, DA_V_DIM), lambda b, h: (b, h)),
                   pl.BlockSpec((N_META, DA_V_DIM), lambda b, h: (b, h))],
        out_shape=[jax.ShapeDtypeStruct((X_ROWS, DA_HEADS * DA_V_DIM), BF16),
                   jax.ShapeDtypeStruct((M_ROWS, DA_HEADS * DA_V_DIM), BF16)],
        scratch_shapes=[pltpu.VMEM((BLOCK_Q, LANES), BF16), pltpu.VMEM((BLOCK_Q, DA_V_DIM), BF16)],
        compiler_params=pltpu.CompilerParams(
            dimension_semantics=("arbitrary", "arbitrary"), vmem_limit_bytes=VMEM_LIMIT),
        name="diff_attn",
    )(qkv_x, qkv_x, qkv_x, qkv_m, qkv_m, qkv_m, lam_p, subln_g)


def _gla_kernel(q_ref, k_ref, v_ref, og_ref, gk_ref, qm_ref, km_ref, vm_ref, ogm_ref, gkm_ref,
                ng_ref, o_ref, om_ref, st_ref, mq_ref, mk_ref, mv_ref, mgk_ref):
    pair = 2 * CHUNK
    ri = lax.broadcasted_iota(jnp.int32, (pair, pair), 0)
    ci = lax.broadcasted_iota(jnp.int32, (pair, pair), 1)
    tri = ((ci <= ri) & ((ri < CHUNK) == (ci < CHUNK))).astype(BF16)
    top = lax.broadcasted_iota(jnp.int32, (pair, GLA_DK), 0) < CHUNK
    ng = ng_ref[...]

    def pair_step(qp, kp, vp, gkp):
        hi = gkp.astype(BF16)
        r1 = gkp - hi.astype(F32)
        mid = r1.astype(BF16)
        lo = (r1 - mid.astype(F32)).astype(BF16)
        G = _dot(tri, hi) + _dot(tri, mid) + _dot(tri, lo)
        g_end_a = G[CHUNK - 1:CHUNK, :]
        g_end_b = G[pair - 1:pair, :]
        k_dec = kp.astype(F32) * jnp.exp(jnp.where(top, g_end_a, g_end_b) - G)
        zero = jnp.zeros_like(k_dec)
        kd_a = jnp.where(top, k_dec, zero).astype(BF16)
        kd_b = jnp.where(top, zero, k_dec).astype(BF16)
        st = st_ref[...] * jnp.exp(g_end_a) + _dot_tn(vp, kd_a)
        o_a = _dot_nt(qp[0:CHUNK], st.astype(BF16))
        st = st * jnp.exp(g_end_b) + _dot_tn(vp, kd_b)
        o_b = _dot_nt(qp[CHUNK:pair], st.astype(BF16))
        st_ref[...] = st
        return jnp.concatenate([o_a, o_b], axis=0)

    def gated(o, og):
        ogf = og.astype(F32)
        return (_rms(o, ng) * (ogf * jax.nn.sigmoid(ogf))).astype(BF16)

    st_ref[...] = jnp.zeros_like(st_ref)

    m0 = CHUNK - N_META
    for ref, src in ((mq_ref, qm_ref), (mk_ref, km_ref), (mv_ref, vm_ref), (mgk_ref, gkm_ref)):
        ref[...] = jnp.zeros_like(ref)
        ref[m0:CHUNK, :] = src[...]
    o_meta = pair_step(mq_ref[...], mk_ref[...], mv_ref[...], mgk_ref[...])
    om_ref[...] = gated(o_meta[m0:CHUNK], ogm_ref[...])

    def body(p, carry):
        rows = pl.ds(pl.multiple_of(p * pair, pair), pair)
        o = pair_step(q_ref[rows, :], k_ref[rows, :], v_ref[rows, :], gk_ref[rows, :])
        o_ref[rows, :] = gated(o, og_ref[rows, :])
        return carry

    lax.fori_loop(0, SEQ // pair, body, 0)


def _gla(proj_x, proj_m, gk_x, gk_m, norm_g):
    kq = GLA_KEY_WIDTH // GLA_DK
    kv = GLA_KEY_WIDTH * 2 // GLA_DV
    ko = kv + GLA_VAL_WIDTH // GLA_DV

    def spec(rows, width, off):
        return pl.BlockSpec((rows, width), lambda b, h: (b, off + h))

    cmap = lambda b, h: (0, 0)
    pair = 2 * CHUNK
    return pl.pallas_call(
        _gla_kernel,
        grid=(BATCH, GLA_HEADS),
        in_specs=[spec(SEQ, GLA_DK, 0), spec(SEQ, GLA_DK, kq), spec(SEQ, GLA_DV, kv),
                  spec(SEQ, GLA_DV, ko), spec(SEQ, GLA_DK, 0),
                  spec(N_META, GLA_DK, 0), spec(N_META, GLA_DK, kq), spec(N_META, GLA_DV, kv),
                  spec(N_META, GLA_DV, ko), spec(N_META, GLA_DK, 0),
                  pl.BlockSpec(norm_g.shape, cmap)],
        out_specs=[spec(SEQ, GLA_DV, 0), spec(N_META, GLA_DV, 0)],
        out_shape=[jax.ShapeDtypeStruct((X_ROWS, GLA_VAL_WIDTH), BF16),
                   jax.ShapeDtypeStruct((M_ROWS, GLA_VAL_WIDTH), BF16)],
        scratch_shapes=[pltpu.VMEM((GLA_DV, GLA_DK), F32),
                        pltpu.VMEM((pair, GLA_DK), BF16), pltpu.VMEM((pair, GLA_DK), BF16),
                        pltpu.VMEM((pair, GLA_DV), BF16), pltpu.VMEM((pair, GLA_DK), F32)],
        compiler_params=pltpu.CompilerParams(
            dimension_semantics=("arbitrary", "arbitrary"), vmem_limit_bytes=VMEM_LIMIT),
        name="gla_scan",
    )(proj_x, proj_x, proj_x, proj_x, gk_x, proj_m, proj_m, proj_m, proj_m, gk_m, norm_g)


def kernel(x, meta, ffn1_norm, ffn1_w_in, ffn1_w_out, mix_norm, ffn2_norm, ffn2_w_in, ffn2_w_out,
           da_w_in, da_lambda, da_subln, da_w_out,
           gla_w_in, gla_w_gate2, gla_b_gate2, gla_norm, gla_w_out, final_norm):
    row = lambda v: v.reshape(1, -1).astype(F32)
    hx = x.reshape(X_ROWS, D_MODEL)
    hm = jnp.broadcast_to(meta[None], (BATCH, N_META, D_MODEL)).reshape(M_ROWS, D_MODEL)

    hx, hm = _ffn(hx, hm, row(ffn1_norm[0]), ffn1_w_in[0].astype(BF16), ffn1_w_out[0].astype(BF16))
    qkv_x, qkv_m = _rowwise_call(
        functools.partial(_proj_body, 1.0, 0), [hx], [hm], [row(mix_norm[0]), da_w_in[0].astype(BF16)],
        [(da_w_in.shape[2], BF16)], [], "da_proj")
    lam_init = 0.8 - 0.6 * math.exp(-0.3 * 0)
    o_x, o_m = _attn(qkv_x, qkv_m, da_lambda[0], row(da_subln[0]), lam_init)
    hx, hm = _rowwise_call(_outproj_body, [o_x, hx], [o_m, hm], [da_w_out[0].astype(BF16)],
                           [(D_MODEL, F32)], [], "da_out")
    hx, hm = _ffn(hx, hm, row(ffn2_norm[0]), ffn2_w_in[0].astype(BF16), ffn2_w_out[0].astype(BF16))

    hx, hm = _ffn(hx, hm, row(ffn1_norm[1]), ffn1_w_in[1].astype(BF16), ffn1_w_out[1].astype(BF16))
    main_w = GLA_KEY_WIDTH * 2 + GLA_VAL_WIDTH * 2
    w_lr = jnp.pad(gla_w_in[0][:, main_w:], ((0, 0), (0, LANES - GLA_GATE_RANK))).astype(BF16)
    w_g2 = jnp.pad(gla_w_gate2[0], ((0, LANES - GLA_GATE_RANK), (0, 0))).astype(BF16)
    proj_x, gk_x, proj_m, gk_m = _rowwise_call(
        _gla_proj_body, [hx], [hm],
        [row(mix_norm[1]), gla_w_in[0][:, :main_w].astype(BF16), w_lr, w_g2, row(gla_b_gate2[0])],
        [(main_w, BF16), (GLA_KEY_WIDTH, F32)], [], "gla_proj")
    o_x, o_m = _gla(proj_x, proj_m, gk_x, gk_m, row(gla_norm[0]))
    hx, hm = _rowwise_call(_outproj_body, [o_x, hx], [o_m, hm], [gla_w_out[0].astype(BF16)],
                           [(D_MODEL, F32)], [], "gla_out")
    (out,) = _ffn(hx, hm, row(ffn2_norm[1]), ffn2_w_in[1].astype(BF16), ffn2_w_out[1].astype(BF16),
                  final_g=row(final_norm))
    return out.reshape(BATCH, SEQ, D_MODEL)
```

```python
import functools
import math

import jax
import jax.numpy as jnp
from jax import lax
from jax.experimental import pallas as pl
from jax.experimental.pallas import tpu as pltpu

F32 = jnp.float32
BF16 = jnp.bfloat16

D_MODEL = 1024
BATCH = 8
SEQ = 2048
N_META = 16
D_FF = 2816
RMS_EPS = 1e-6
NEG_INF = -1e30
CHUNK = 64
BLOCK_Q = 128
ATTN_HEADS_PER_STEP = 4

DA_HEADS = 8
DA_HEAD_DIM = 64
DA_V_DIM = 128
DA_QK_WIDTH = 1024

GLA_HEADS = 4
GLA_KEY_WIDTH = 512
GLA_VAL_WIDTH = 1024
GLA_DK = 128
GLA_DV = 256
GLA_GATE_RANK = 16
GLA_GATE_NORM = 16.0
GLA_HEADS_PER_STEP = 2

X_ROWS = BATCH * SEQ
M_ROWS = BATCH * N_META
LANES = 128

ROW_TILE = 512
PROJ_ROW_TILE = 1024
ROW_SUB = 512
FF_TILE = 256
PROJ_TILE = 512
VMEM_LIMIT = 56 * 1024 * 1024


def _rms(x, g):
    return x * lax.rsqrt(jnp.mean(x * x, axis=-1, keepdims=True) + RMS_EPS) * g


def _dot(a, b):
    return jnp.dot(a, b, preferred_element_type=F32)


def _dot_nt(a, b):
    return lax.dot_general(a, b, (((1,), (1,)), ((), ())), preferred_element_type=F32)


def _dot_tn(a, b):
    return lax.dot_general(a, b, (((0,), (0,)), ((), ())), preferred_element_type=F32)


def _row_subtiles(rows):
    sub = min(ROW_SUB, rows)
    return [(r0, r0 + sub) for r0 in range(0, rows, sub)]


N_FF_CHUNKS = D_FF // FF_TILE


def _ffn_kernel(nx, final, mixed, with_meta, *refs):
    refs = list(refs)
    take = lambda n: [refs.pop(0) for _ in range(n)]
    xs = take(2 if mixed else 1)
    ms = take(len(xs)) if with_meta else []
    g_ref, wg_ref, wu_ref, wo_ref = take(4)
    (wmix_ref,) = take(1) if mixed else (None,)
    (fg_ref,) = take(1) if final else (None,)
    (ox_ref,) = take(1)
    (om_ref,) = take(1) if with_meta else (None,)
    win_ref, wout_ref, act_ref, xn_ref = refs
    nf = N_FF_CHUNKS
    s = pl.program_id(0)

    def head(ins, out_ref, r0, r1):
        if mixed:
            o_ref, h_ref = ins
            out_ref[r0:r1, :] = h_ref[r0:r1, :] + _dot(o_ref[r0:r1, :], wmix_ref[...])
            h1_ref = out_ref
        else:
            (h1_ref,) = ins
        xn_ref[r0:r1, :] = _rms(h1_ref[r0:r1, :], g_ref[...]).astype(BF16)
        return h1_ref

    def chunk(j, r0, r1):
        lo = j * FF_TILE
        xn = xn_ref[r0:r1, :]
        gate = _dot(xn, win_ref[:, lo:lo + FF_TILE])
        up = _dot(xn, win_ref[:, D_FF + lo:D_FF + lo + FF_TILE])
        act_ref[r0:r1, lo:lo + FF_TILE] = (gate * jax.nn.sigmoid(gate) * up).astype(BF16)

    def tail(h1_ref, out_ref, r0, r1):
        y = h1_ref[r0:r1, :] + 0.5 * _dot(act_ref[r0:r1, :], wout_ref[...])
        if final:
            y = _rms(y, fg_ref[...])
        out_ref[r0:r1, :] = y

    def rows_pass(ins, out_ref, rows):
        for r0, r1 in _row_subtiles(rows):
            h1_ref = head(ins, out_ref, r0, r1)
            for j in range(nf):
                chunk(j, r0, r1)
            tail(h1_ref, out_ref, r0, r1)

    for j in range(nf):
        @pl.when(s == j)
        def _(j=j):
            lo = j * FF_TILE
            win_ref[:, lo:lo + FF_TILE] = wg_ref[...].astype(BF16)
            win_ref[:, D_FF + lo:D_FF + lo + FF_TILE] = wu_ref[...].astype(BF16)
            wout_ref[lo:lo + FF_TILE, :] = wo_ref[...].astype(BF16)
            if j == 0:
                head(xs, ox_ref, 0, ROW_TILE)
            chunk(j, 0, ROW_TILE)
            if j == nf - 1:
                tail(ox_ref if mixed else xs[0], ox_ref, 0, ROW_TILE)

    @pl.when((s >= nf) & (s < nf + nx - 1))
    def _():
        rows_pass(xs, ox_ref, ROW_TILE)

    if with_meta:
        @pl.when(s == nf + nx - 1)
        def _():
            rows_pass(ms, om_ref, M_ROWS)


def _ffn(hx, hm, g, w_in, w_out, layer, mix=None, final_g=None):
    final, mixed = final_g is not None, mix is not None
    with_meta = not final
    bm, nf = ROW_TILE, N_FF_CHUNKS
    nx = X_ROWS // bm
    tmap = lambda s: (jnp.clip(s - (nf - 1), 0, nx - 1), 0)
    cmap = lambda s: (0, 0)
    cidx = lambda s: jnp.minimum(s, nf - 1)
    xs = [mix[0], hx] if mixed else [hx]
    ms = [mix[1], hm] if mixed else [hm]
    args = list(xs)
    in_specs = [pl.BlockSpec((bm, a.shape[1]), tmap) for a in xs]
    if with_meta:
        args += ms
        in_specs += [pl.BlockSpec((M_ROWS, a.shape[1]), cmap) for a in ms]
    args += [g, w_in, w_in, w_out]
    in_specs += [pl.BlockSpec(g.shape, cmap, pipeline_mode=pl.Buffered(1)),
                 pl.BlockSpec((None, D_MODEL, FF_TILE), lambda s: (layer, 0, cidx(s))),
                 pl.BlockSpec((None, D_MODEL, FF_TILE), lambda s: (layer, 0, nf + cidx(s))),
                 pl.BlockSpec((None, FF_TILE, D_MODEL), lambda s: (layer, cidx(s), 0))]
    if mixed:
        w_mix, mix_layer = mix[2]
        args.append(w_mix)
        in_specs.append(pl.BlockSpec((None,) + w_mix.shape[1:], lambda s: (mix_layer, 0, 0),
                                     pipeline_mode=pl.Buffered(1)))
    if final:
        args.append(final_g)
        in_specs.append(pl.BlockSpec(final_g.shape, cmap, pipeline_mode=pl.Buffered(1)))
    out_specs = [pl.BlockSpec((bm, D_MODEL), tmap)]
    out_shape = [jax.ShapeDtypeStruct((X_ROWS, D_MODEL), F32)]
    if with_meta:
        out_specs.append(pl.BlockSpec((M_ROWS, D_MODEL), cmap))
        out_shape.append(jax.ShapeDtypeStruct((M_ROWS, D_MODEL), F32))
    return pl.pallas_call(
        functools.partial(_ffn_kernel, nx, final, mixed, with_meta),
        grid=(nf + nx - 1 + (1 if with_meta else 0),),
        in_specs=in_specs,
        out_specs=out_specs,
        out_shape=out_shape,
        scratch_shapes=[pltpu.VMEM((D_MODEL, 2 * D_FF), BF16), pltpu.VMEM((D_FF, D_MODEL), BF16),
                        pltpu.VMEM((bm, D_FF), BF16), pltpu.VMEM((bm, D_MODEL), BF16)],
        compiler_params=pltpu.CompilerParams(
            dimension_semantics=("arbitrary",), vmem_limit_bytes=VMEM_LIMIT),
        name="ffn" + ("_mix" if mixed else "") + ("_final" if final else ""),
    )(*args)


def _proj_kernel(nx, nc, q_scale, q_width, gated, *refs):
    refs = list(refs)
    take = lambda n: [refs.pop(0) for _ in range(n)]
    hx_ref, hm_ref, g_ref, wc_ref = take(4)
    wlr_ref, wg2_ref, bg2_ref = take(3) if gated else (None, None, None)
    ox_ref, om_ref = take(2)
    gkx_ref, gkm_ref = take(2) if gated else (None, None)
    w_ref, xn_ref = refs
    s = pl.program_id(0)

    def head(h_ref, gk_ref, r0, r1):
        xn = _rms(h_ref[r0:r1, :], g_ref[...]).astype(BF16)
        xn_ref[r0:r1, :] = xn
        if gated:
            glr = _dot(xn, wlr_ref[...]).astype(BF16)
            z = _dot(glr, wg2_ref[...]) + bg2_ref[...]
            log_sig = jnp.minimum(z, 0.0) - jnp.log(1.0 + jnp.exp(-jnp.abs(z)))
            gk_ref[r0:r1, :] = log_sig / GLA_GATE_NORM

    def chunk(c, out_ref, r0, r1):
        lo = c * PROJ_TILE
        y = _dot(xn_ref[r0:r1, :], w_ref[:, lo:lo + PROJ_TILE])
        if lo < q_width:
            y = y * q_scale
        out_ref[r0:r1, lo:lo + PROJ_TILE] = y.astype(BF16)

    def rows_pass(h_ref, out_ref, gk_ref, rows):
        for r0, r1 in _row_subtiles(rows):
            head(h_ref, gk_ref, r0, r1)
            for c in range(nc):
                chunk(c, out_ref, r0, r1)

    for c in range(nc):
        @pl.when(s == c)
        def _(c=c):
            w_ref[:, c * PROJ_TILE:(c + 1) * PROJ_TILE] = wc_ref[...].astype(BF16)
            if c == 0:
                head(hx_ref, gkx_ref, 0, PROJ_ROW_TILE)
            chunk(c, ox_ref, 0, PROJ_ROW_TILE)

    @pl.when((s >= nc) & (s < nc + nx - 1))
    def _():
        rows_pass(hx_ref, ox_ref, gkx_ref, PROJ_ROW_TILE)

    @pl.when(s == nc + nx - 1)
    def _():
        rows_pass(hm_ref, om_ref, gkm_ref, M_ROWS)


def _proj(hx, hm, g, w, layer, width, q_scale, q_width, name, gate=None):
    gated = gate is not None
    bm, nc = PROJ_ROW_TILE, width // PROJ_TILE
    nx = X_ROWS // bm
    tmap = lambda s: (jnp.clip(s - (nc - 1), 0, nx - 1), 0)
    cmap = lambda s: (0, 0)
    const = lambda a: pl.BlockSpec(a.shape, cmap, pipeline_mode=pl.Buffered(1))
    args = [hx, hm, g, w] + (list(gate) if gated else [])
    in_specs = [pl.BlockSpec((bm, D_MODEL), tmap), pl.BlockSpec((M_ROWS, D_MODEL), cmap), const(g),
                pl.BlockSpec((None, D_MODEL, PROJ_TILE), lambda s: (layer, 0, jnp.minimum(s, nc - 1)))]
    in_specs += [const(a) for a in gate] if gated else []
    out_specs = [pl.BlockSpec((bm, width), tmap), pl.BlockSpec((M_ROWS, width), cmap)]
    out_shape = [jax.ShapeDtypeStruct((X_ROWS, width), BF16), jax.ShapeDtypeStruct((M_ROWS, width), BF16)]
    if gated:
        out_specs += [pl.BlockSpec((bm, GLA_KEY_WIDTH), tmap), pl.BlockSpec((M_ROWS, GLA_KEY_WIDTH), cmap)]
        out_shape += [jax.ShapeDtypeStruct((X_ROWS, GLA_KEY_WIDTH), F32),
                      jax.ShapeDtypeStruct((M_ROWS, GLA_KEY_WIDTH), F32)]
    return pl.pallas_call(
        functools.partial(_proj_kernel, nx, nc, q_scale, q_width, gated),
        grid=(nc + nx,),
        in_specs=in_specs,
        out_specs=out_specs,
        out_shape=out_shape,
        scratch_shapes=[pltpu.VMEM((D_MODEL, width), BF16), pltpu.VMEM((bm, D_MODEL), BF16)],
        compiler_params=pltpu.CompilerParams(
            dimension_semantics=("arbitrary",), vmem_limit_bytes=VMEM_LIMIT),
        name=name,
    )(*args)


def _attn_kernel(lam_init, q_ref, k_ref, v_ref, qm_ref, km_ref, vm_ref, lam_ref, sg_ref,
                 o_ref, om_ref, vt_ref, vpad_ref, vtm_ref, s_ref, p_ref, pm_ref):
    nq = BLOCK_Q
    lam = (jnp.exp(jnp.sum(lam_ref[0:1, :] * lam_ref[1:2, :], axis=-1, keepdims=True))
           - jnp.exp(jnp.sum(lam_ref[2:3, :] * lam_ref[3:4, :], axis=-1, keepdims=True))
           + lam_init)
    sg = sg_ref[...] * (1.0 - lam_init)
    first = lax.broadcasted_iota(jnp.int32, (nq, LANES), 1) < DA_HEAD_DIM
    early_query = (lax.broadcasted_iota(jnp.int32, (CHUNK, 2 * nq), 1) & (nq - 1)) < CHUNK
    heads = range(ATTN_HEADS_PER_STEP)
    cols = lambda h: slice(h * LANES, (h + 1) * LANES)

    for h in heads:
        for t in range(SEQ // nq):
            vt_ref[h, :, t * nq:(t + 1) * nq] = v_ref[t * nq:(t + 1) * nq, cols(h)].T
        vpad_ref[h] = jnp.zeros((nq, DA_V_DIM), BF16)
        vpad_ref[h, 0:N_META, :] = vm_ref[:, cols(h)]
        vtm_ref[h] = vpad_ref[h].T
        pm_ref[h] = jnp.zeros((nq, 2 * nq), BF16)

    def stack_maps(qb):
        zero = jnp.zeros_like(qb)
        return jnp.concatenate([jnp.where(first, qb, zero), jnp.where(first, zero, qb)], axis=0)

    def scores(h, qs, e, slot):
        s_m = _dot_nt(km_ref[:, cols(h)], qs)
        m = s_m.max(axis=0, keepdims=True)
        if e:
            s_x = _dot_nt(k_ref[0:e, cols(h)], qs)
            tail = jnp.where(early_query, NEG_INF, s_x[e - CHUNK:e])
            s_ref[h, slot, 0:e - CHUNK, :] = s_x[0:e - CHUNK]
            s_ref[h, slot, e - CHUNK:e, :] = tail
            m = jnp.maximum(m, jnp.maximum(s_x[0:e - CHUNK].max(axis=0, keepdims=True),
                                           tail.max(axis=0, keepdims=True)))
        return s_m, m

    def probs(h, s_m, m, e, slot):
        p_m = jnp.exp2(s_m - m)
        den = p_m.sum(axis=0, keepdims=True)
        pm_ref[h, 0:N_META, :] = p_m.astype(BF16)
        if e:
            p_x = jnp.exp2(s_ref[h, slot, 0:e, :] - m)
            den = den + p_x.sum(axis=0, keepdims=True)
            p_ref[h, slot, 0:e, :] = p_x.astype(BF16)
        return den

    def output(h, den, e, slot):
        res = _dot(vtm_ref[h], pm_ref[h])
        if e:
            res = res + _dot(vt_ref[h, :, 0:e], p_ref[h, slot, 0:e, :])
        r = 1.0 / den
        o_t = res[:, 0:nq] * r[:, 0:nq] - res[:, nq:2 * nq] * (lam * r[:, nq:2 * nq])
        return _rms(o_t.T, sg).astype(BF16)

    n_blocks = SEQ // nq
    stacked = lambda h, j: stack_maps(q_ref[j * nq:(j + 1) * nq, cols(h)])
    state = [scores(h, stacked(h, 0), nq, 0) for h in heads]
    for j in range(n_blocks):
        e, slot = (j + 1) * nq, j % 2
        nxt = [scores(h, stacked(h, j + 1), e + nq, 1 - slot) if j + 1 < n_blocks else None
               for h in heads]
        dens = [probs(h, *state[h], e, slot) for h in heads]
        for h in heads:
            o_ref[j * nq:(j + 1) * nq, cols(h)] = output(h, dens[h], e, slot)
        state = nxt

    pad = jnp.zeros((nq - N_META, LANES), BF16)
    fm = lax.broadcasted_iota(jnp.int32, (N_META, LANES), 1) < DA_HEAD_DIM
    for h in heads:
        qm = qm_ref[:, cols(h)]
        zq = jnp.zeros_like(qm)
        qs_m = jnp.concatenate([jnp.where(fm, qm, zq), pad, jnp.where(fm, zq, qm), pad], axis=0)
        s_m, m = scores(h, qs_m, 0, 0)
        den = probs(h, s_m, m, 0, 0)
        om_ref[:, cols(h)] = output(h, den, 0, 0)[0:N_META]


def _attn(qkv_x, qkv_m, lam_p, subln_g, lam_init):
    hps = ATTN_HEADS_PER_STEP
    nqk = DA_QK_WIDTH // (hps * LANES)
    xspec = lambda off: pl.BlockSpec((SEQ, hps * LANES), lambda b, h: (b, off + h))
    mspec = lambda off: pl.BlockSpec((N_META, hps * LANES), lambda b, h: (b, off + h))
    cmap = lambda b, h: (0, 0)
    return pl.pallas_call(
        functools.partial(_attn_kernel, lam_init),
        grid=(BATCH, DA_HEADS // hps),
        in_specs=[xspec(0), xspec(nqk), xspec(2 * nqk), mspec(0), mspec(nqk), mspec(2 * nqk),
                  pl.BlockSpec(lam_p.shape, cmap), pl.BlockSpec(subln_g.shape, cmap)],
        out_specs=[pl.BlockSpec((SEQ, hps * DA_V_DIM), lambda b, h: (b, h)),
                   pl.BlockSpec((N_META, hps * DA_V_DIM), lambda b, h: (b, h))],
        out_shape=[jax.ShapeDtypeStruct((X_ROWS, DA_HEADS * DA_V_DIM), BF16),
                   jax.ShapeDtypeStruct((M_ROWS, DA_HEADS * DA_V_DIM), BF16)],
        scratch_shapes=[pltpu.VMEM((hps, DA_V_DIM, SEQ), BF16),
                        pltpu.VMEM((hps, BLOCK_Q, DA_V_DIM), BF16),
                        pltpu.VMEM((hps, DA_V_DIM, BLOCK_Q), BF16),
                        pltpu.VMEM((hps, 2, SEQ, 2 * BLOCK_Q), F32),
                        pltpu.VMEM((hps, 2, SEQ, 2 * BLOCK_Q), BF16),
                        pltpu.VMEM((hps, BLOCK_Q, 2 * BLOCK_Q), BF16)],
        compiler_params=pltpu.CompilerParams(
            dimension_semantics=("arbitrary", "arbitrary"), vmem_limit_bytes=VMEM_LIMIT),
        name="diff_attn",
    )(qkv_x, qkv_x, qkv_x, qkv_m, qkv_m, qkv_m, lam_p, subln_g)


def _gla_kernel(q_ref, k_ref, v_ref, og_ref, gk_ref, qm_ref, km_ref, vm_ref, ogm_ref, gkm_ref,
                ng_ref, o_ref, om_ref, kd_ref, dec_ref, st_ref, mq_ref, mk_ref, mv_ref, mgk_ref):
    pair = 2 * CHUNK
    n_chunks = SEQ // CHUNK
    ri = lax.broadcasted_iota(jnp.int32, (pair, pair), 0)
    ci = lax.broadcasted_iota(jnp.int32, (pair, pair), 1)
    tri = ((ci <= ri) & ((ri < CHUNK) == (ci < CHUNK))).astype(BF16)
    top = lax.broadcasted_iota(jnp.int32, (pair, GLA_DK), 0) < CHUNK
    ng = ng_ref[...]
    heads = range(GLA_HEADS_PER_STEP)
    kc = lambda h: slice(h * GLA_DK, (h + 1) * GLA_DK)
    vc = lambda h: slice(h * GLA_DV, (h + 1) * GLA_DV)

    def decayed_keys(kp, gkp):
        hi = gkp.astype(BF16)
        r1 = gkp - hi.astype(F32)
        mid = r1.astype(BF16)
        lo = (r1 - mid.astype(F32)).astype(BF16)
        G = _dot(tri, hi) + _dot(tri, mid) + _dot(tri, lo)
        g_end_a = G[CHUNK - 1:CHUNK, :]
        g_end_b = G[pair - 1:pair, :]
        k_dec = kp.astype(F32) * jnp.exp(jnp.where(top, g_end_a, g_end_b) - G)
        return k_dec.astype(BF16), jnp.exp(g_end_a), jnp.exp(g_end_b)

    def gated(o, og):
        ogf = og.astype(F32)
        return (_rms(o, ng) * (ogf * jax.nn.sigmoid(ogf))).astype(BF16)

    m0 = CHUNK - N_META
    st = []
    for h in heads:
        for ref, src, cs in ((mq_ref, qm_ref, kc(h)), (mk_ref, km_ref, kc(h)), (mv_ref, vm_ref, vc(h)),
                             (mgk_ref, gkm_ref, kc(h))):
            ref[h] = jnp.zeros(ref.shape[1:], ref.dtype)
            ref[h, m0:CHUNK, :] = src[:, cs]
        kd_m, _, _ = decayed_keys(mk_ref[h], mgk_ref[h])
        st.append(_dot_tn(mv_ref[h], kd_m))
        o_meta = _dot_nt(mq_ref[h, 0:CHUNK, :], st[h].astype(BF16))
        om_ref[:, vc(h)] = gated(o_meta[m0:CHUNK], ogm_ref[:, vc(h)])

    for p in range(SEQ // pair):
        rows = slice(p * pair, (p + 1) * pair)
        for h in heads:
            kd, dec_a, dec_b = decayed_keys(k_ref[rows, kc(h)], gk_ref[rows, kc(h)])
            kd_ref[h, rows, :] = kd
            dec_ref[h, 2 * p:2 * p + 1, :] = dec_a
            dec_ref[h, 2 * p + 1:2 * p + 2, :] = dec_b

    for c in range(n_chunks):
        rows = slice(c * CHUNK, (c + 1) * CHUNK)
        for h in heads:
            st[h] = st[h] * dec_ref[h, c:c + 1, :] + _dot_tn(v_ref[rows, vc(h)], kd_ref[h, rows, :])
            st_ref[h, c] = st[h].astype(BF16)

    for p in range(SEQ // pair):
        rows = slice(p * pair, (p + 1) * pair)
        for h in heads:
            o = jnp.concatenate(
                [_dot_nt(q_ref[p * pair:p * pair + CHUNK, kc(h)], st_ref[h, 2 * p]),
                 _dot_nt(q_ref[p * pair + CHUNK:(p + 1) * pair, kc(h)], st_ref[h, 2 * p + 1])], axis=0)
            o_ref[rows, vc(h)] = gated(o, og_ref[rows, vc(h)])


def _gla(proj_x, proj_m, gk_x, gk_m, norm_g):
    hps = GLA_HEADS_PER_STEP
    wk, wv = hps * GLA_DK, hps * GLA_DV
    kq = GLA_KEY_WIDTH // wk
    kv = GLA_KEY_WIDTH * 2 // wv
    ko = kv + GLA_VAL_WIDTH // wv

    def spec(rows, width, off):
        return pl.BlockSpec((rows, width), lambda b, h: (b, off + h))

    cmap = lambda b, h: (0, 0)
    pair = 2 * CHUNK
    return pl.pallas_call(
        _gla_kernel,
        grid=(BATCH, GLA_HEADS // hps),
        in_specs=[spec(SEQ, wk, 0), spec(SEQ, wk, kq), spec(SEQ, wv, kv),
                  spec(SEQ, wv, ko), spec(SEQ, wk, 0),
                  spec(N_META, wk, 0), spec(N_META, wk, kq), spec(N_META, wv, kv),
                  spec(N_META, wv, ko), spec(N_META, wk, 0),
                  pl.BlockSpec(norm_g.shape, cmap)],
        out_specs=[spec(SEQ, wv, 0), spec(N_META, wv, 0)],
        out_shape=[jax.ShapeDtypeStruct((X_ROWS, GLA_VAL_WIDTH), BF16),
                   jax.ShapeDtypeStruct((M_ROWS, GLA_VAL_WIDTH), BF16)],
        scratch_shapes=[pltpu.VMEM((hps, SEQ, GLA_DK), BF16),
                        pltpu.VMEM((hps, SEQ // CHUNK, GLA_DK), F32),
                        pltpu.VMEM((hps, SEQ // CHUNK, GLA_DV, GLA_DK), BF16),
                        pltpu.VMEM((hps, pair, GLA_DK), BF16), pltpu.VMEM((hps, pair, GLA_DK), BF16),
                        pltpu.VMEM((hps, pair, GLA_DV), BF16), pltpu.VMEM((hps, pair, GLA_DK), F32)],
        compiler_params=pltpu.CompilerParams(
            dimension_semantics=("arbitrary", "arbitrary"), vmem_limit_bytes=VMEM_LIMIT),
        name="gla_scan",
    )(proj_x, proj_x, proj_x, proj_x, gk_x, proj_m, proj_m, proj_m, proj_m, gk_m, norm_g)


def kernel(x, meta, ffn1_norm, ffn1_w_in, ffn1_w_out, mix_norm, ffn2_norm, ffn2_w_in, ffn2_w_out,
           da_w_in, da_lambda, da_subln, da_w_out,
           gla_w_in, gla_w_gate2, gla_b_gate2, gla_norm, gla_w_out, final_norm):
    row = lambda v: v.reshape(1, -1).astype(F32)
    hx = x.reshape(X_ROWS, D_MODEL)
    hm = jnp.broadcast_to(meta[None], (BATCH, N_META, D_MODEL)).reshape(M_ROWS, D_MODEL)

    da_out, gla_out = da_w_out.astype(BF16), gla_w_out.astype(BF16)

    hx, hm = _ffn(hx, hm, row(ffn1_norm[0]), ffn1_w_in, ffn1_w_out, 0)
    qkv_x, qkv_m = _proj(hx, hm, row(mix_norm[0]), da_w_in, 0, da_w_in.shape[2],
                         DA_HEAD_DIM ** -0.5 * math.log2(math.e), DA_QK_WIDTH, "da_proj")
    lam_init = 0.8 - 0.6 * math.exp(-0.3 * 0)
    o_x, o_m = _attn(qkv_x, qkv_m, da_lambda[0], row(da_subln[0]), lam_init)
    hx, hm = _ffn(hx, hm, row(ffn2_norm[0]), ffn2_w_in, ffn2_w_out, 0, mix=(o_x, o_m, (da_out, 0)))

    hx, hm = _ffn(hx, hm, row(ffn1_norm[1]), ffn1_w_in, ffn1_w_out, 1)
    main_w = GLA_KEY_WIDTH * 2 + GLA_VAL_WIDTH * 2
    w_lr = jnp.pad(gla_w_in[0, :, main_w:], ((0, 0), (0, LANES - GLA_GATE_RANK))).astype(BF16)
    w_g2 = jnp.pad(gla_w_gate2[0], ((0, LANES - GLA_GATE_RANK), (0, 0))).astype(BF16)
    proj_x, proj_m, gk_x, gk_m = _proj(hx, hm, row(mix_norm[1]), gla_w_in, 0, main_w,
                                       GLA_DK ** -0.5, GLA_KEY_WIDTH, "gla_proj",
                                       gate=(w_lr, w_g2, row(gla_b_gate2[0])))
    o_x, o_m = _gla(proj_x, proj_m, gk_x, gk_m, row(gla_norm[0]))
    (out,) = _ffn(hx, hm, row(ffn2_norm[1]), ffn2_w_in, ffn2_w_out, 1, mix=(o_x, o_m, (gla_out, 0)),
                  final_g=row(final_norm))
    return out.reshape(BATCH, SEQ, D_MODEL)
```
